```python
import math
import jax, jax.numpy as jnp
from jax import lax
import numpy as np

D_MODEL = 4096
BATCH = 4
SEQ = 4096
DEPTH = 2
DEC_BATCH = 16
DEC_SEQ = 32
PAST_LEN = 2048

CHUNK = 64
N_MIXERS = 2
N_A = (DEPTH + 1) // 2
N_B = DEPTH // 2
MLSTM_PF = 2
INNER = MLSTM_PF * D_MODEL
N_HEADS = 8
DK = INNER // N_HEADS
DV = INNER // N_HEADS
QKV_BLOCK = 4
N_QKV_BLOCKS = INNER // QKV_BLOCK
MCONV_W = 4
CONV_CH = D_MODEL
CONV_W = 31
ALPHA = (2 * DEPTH) ** 0.25
BETA = (8 * DEPTH) ** -0.25
LN_EPS = 1e-5

kernel_name = "mlstm_conformer_conv_stream_step"


def _layernorm(x, g, b=None):
    xf = x.astype(jnp.float32)
    mu = jnp.mean(xf, axis=-1, keepdims=True)
    var = jnp.mean(jnp.square(xf - mu), axis=-1, keepdims=True)
    y = (xf - mu) * lax.rsqrt(var + LN_EPS) * g.astype(jnp.float32)
    if b is not None:
        y = y + b.astype(jnp.float32)
    return y.astype(x.dtype)


def _causal_dwconv(x, hist, w, b):
    W = w.shape[0]
    T = x.shape[1]
    xp = jnp.concatenate([hist.astype(x.dtype), x], axis=1)
    wc = w.astype(x.dtype)
    y = sum(xp[:, j:j + T] * wc[j] for j in range(W))
    return y + b.astype(x.dtype), xp[:, xp.shape[1] - (W - 1):]


def _headwise(x, w):
    B, T, _ = x.shape
    xb = x.reshape(B, T, N_QKV_BLOCKS, QKV_BLOCK)
    return jnp.einsum('btnd,nde->btne', xb, w.astype(x.dtype)).reshape(B, T, INNER)


def _mlstm_cell(q, k, v, ig, lf, C0, n0, m0):
    B, H, T, _ = q.shape
    L = min(CHUNK, T)
    NC = T // L

    def to_chunks(a):
        return jnp.moveaxis(a.reshape(a.shape[:2] + (NC, L) + a.shape[3:]), 2, 0)

    causal = jnp.tril(jnp.ones((L, L), dtype=bool))

    def step(carry, inp):
        C, n, m = carry
        qc, kc, vc, ic, fc = inp
        b = jnp.cumsum(fc, axis=-1)
        logD = jnp.where(causal, b[..., :, None] - b[..., None, :] + ic[..., None, :], -jnp.inf)
        g = b + m[..., None]
        m_t = jnp.maximum(g, jnp.max(logD, axis=-1))
        Dm = jnp.exp(logD - m_t[..., None])
        inter = jnp.exp(g - m_t)
        S = jnp.einsum('bhtd,bhsd->bhts', qc, kc) * Dm
        num = jnp.einsum('bhts,bhsv->bhtv', S, vc) + inter[..., None] * jnp.einsum('bhtd,bhdv->bhtv', qc, C)
        den = jnp.sum(S, axis=-1) + inter * jnp.einsum('bhtd,bhd->bht', qc, n)
        h = num / jnp.maximum(jnp.abs(den), jnp.exp(-m_t))[..., None]
        bL = b[..., -1]
        logw = bL[..., None] - b + ic
        decay = bL + m
        m_new = jnp.maximum(decay, jnp.max(logw, axis=-1))
        wk = kc * jnp.exp(logw - m_new[..., None])[..., None]
        sc = jnp.exp(decay - m_new)
        C_new = sc[..., None, None] * C + jnp.einsum('bhsd,bhsv->bhdv', wk, vc)
        n_new = sc[..., None] * n + jnp.sum(wk, axis=2)
        return (C_new, n_new, m_new), h

    (C, n, m), hs = lax.scan(step, (C0, n0, m0), tuple(map(to_chunks, (q, k, v, ig, lf))))
    h = jnp.moveaxis(hs, 0, 2).reshape(B, H, T, DV)
    return h, C, n, m


def _mlstm_mixer(x, C0, n0, m0, hist, w_up, w_mconv, b_mconv, w_q, w_k, w_v, w_gate, b_gate, mh_gain, skip, w_down):
    B, T, _ = x.shape
    f32 = jnp.float32
    up = x @ w_up.astype(x.dtype)
    xm, z = jnp.split(up, 2, axis=-1)
    xc, hist_new = _causal_dwconv(xm, hist, w_mconv, b_mconv)
    xa = jax.nn.silu(xc)
    q = _headwise(xa, w_q)
    k = _headwise(xa, w_k)
    v = _headwise(xm, w_v)
    wg = w_gate.astype(x.dtype)
    gates = (q @ wg[0] + k @ wg[1] + v @ wg[2]).astype(f32) + b_gate.astype(f32)
    ig = jnp.transpose(gates[..., :N_HEADS], (0, 2, 1))
    lf = jnp.transpose(jax.nn.log_sigmoid(gates[..., N_HEADS:]), (0, 2, 1))

    def heads(a):
        return a.reshape(B, T, N_HEADS, -1).transpose(0, 2, 1, 3).astype(f32)

    h, C, n, m = _mlstm_cell(heads(q), heads(k) * (DK ** -0.5), heads(v), ig, lf,
                             C0.astype(f32), n0.astype(f32), m0.astype(f32))
    hn = _layernorm(h, mh_gain.reshape(N_HEADS, 1, DV))
    hn = hn.transpose(0, 2, 1, 3).reshape(B, T, INNER).astype(x.dtype)
    out = ((hn + skip.astype(x.dtype) * xa) * jax.nn.silu(z)) @ w_down.astype(x.dtype)
    return out, C, n, m, hist_new


def _conv_mixer(x, hist, w_cin, b_cin, w_dw, b_dw, cln_g, cln_b, w_cout, b_cout):
    proj = x @ w_cin.astype(x.dtype) + b_cin.astype(x.dtype)
    a, gl, zg = jnp.split(proj, 3, axis=-1)
    u = a * jax.nn.sigmoid(gl)
    c, hist_new = _causal_dwconv(u, hist, w_dw, b_dw)
    c = jax.nn.silu(_layernorm(c, cln_g, cln_b))
    out = (c * jax.nn.silu(zg)) @ w_cout.astype(x.dtype) + b_cout.astype(x.dtype)
    return out, hist_new


def _trunk(x, st_C, st_n, st_m, st_mconv, st_cconv,
           w_up, w_mconv, b_mconv, w_q, w_k, w_v, w_gate, b_gate, mh_gain, skip, w_down,
           w_cin, b_cin, w_dw, b_dw, cln_g, cln_b, w_cout, b_cout, post_ln_g, post_ln_b):
    Cs, ns, ms, mcs, ccs = [], [], [], [], []
    for i in range(DEPTH):
        j = i // N_MIXERS
        if i % N_MIXERS == 0:
            out, C, n, m, hc = _mlstm_mixer(x, st_C[j], st_n[j], st_m[j], st_mconv[j],
                                            w_up[j], w_mconv[j], b_mconv[j], w_q[j], w_k[j], w_v[j],
                                            w_gate[j], b_gate[j], mh_gain[j], skip[j], w_down[j])
            Cs.append(C); ns.append(n); ms.append(m); mcs.append(hc)
        else:
            out, hc = _conv_mixer(x, st_cconv[j], w_cin[j], b_cin[j], w_dw[j], b_dw[j],
                                  cln_g[j], cln_b[j], w_cout[j], b_cout[j])
            ccs.append(hc)
        x = _layernorm(ALPHA * x + out, post_ln_g[i], post_ln_b[i])
    dt = x.dtype
    return (x, jnp.stack(Cs).astype(dt), jnp.stack(ns).astype(dt), jnp.stack(ms).astype(dt),
            jnp.stack(mcs).astype(dt), jnp.stack(ccs).astype(dt))


def setup_inputs(seed: int = 0) -> dict:
    key = jax.random.key(seed)
    ks = jax.random.split(key, 32)
    nrm = jax.random.normal
    f = jnp.float32
    b_gate = jnp.concatenate([
        0.1 * nrm(ks[10], (N_A, N_HEADS), f),
        jnp.broadcast_to(jnp.linspace(3.0, 6.0, N_HEADS, dtype=f), (N_A, N_HEADS)) + 0.01 * nrm(ks[11], (N_A, N_HEADS), f),
    ], axis=-1)
    return {
        "x_prompt": nrm(ks[0], (BATCH, SEQ, D_MODEL), f),
        "x_sample": nrm(ks[1], (DEC_BATCH, DEC_SEQ, D_MODEL), f),
        "state_mlstm_C": 0.02 * nrm(ks[2], (N_A, DEC_BATCH, N_HEADS, DK, DV), f),
        "state_mlstm_n": 0.02 * nrm(ks[3], (N_A, DEC_BATCH, N_HEADS, DK), f),
        "state_mlstm_m": nrm(ks[4], (N_A, DEC_BATCH, N_HEADS), f),
        "state_mlstm_conv": nrm(ks[5], (N_A, DEC_BATCH, MCONV_W - 1, INNER), f),
        "state_conformer_conv": 0.5 * nrm(ks[6], (N_B, DEC_BATCH, CONV_W - 1, CONV_CH), f),
        "w_up": nrm(ks[7], (N_A, D_MODEL, 2 * INNER), f) * D_MODEL ** -0.5,
        "w_mconv": nrm(ks[8], (N_A, MCONV_W, INNER), f) * MCONV_W ** -0.5,
        "b_mconv": 0.01 * nrm(ks[9], (N_A, INNER), f),
        "w_q": nrm(ks[12], (N_A, N_QKV_BLOCKS, QKV_BLOCK, QKV_BLOCK), f) * QKV_BLOCK ** -0.5,
        "w_k": nrm(ks[13], (N_A, N_QKV_BLOCKS, QKV_BLOCK, QKV_BLOCK), f) * QKV_BLOCK ** -0.5,
        "w_v": nrm(ks[14], (N_A, N_QKV_BLOCKS, QKV_BLOCK, QKV_BLOCK), f) * QKV_BLOCK ** -0.5,
        "w_gate": nrm(ks[15], (N_A, 3, INNER, 2 * N_HEADS), f) * (0.1 * (3 * INNER) ** -0.5),
        "b_gate": b_gate,
        "mh_gain": 1.0 + 0.01 * nrm(ks[16], (N_A, INNER), f),
        "skip": 1.0 + 0.01 * nrm(ks[17], (N_A, INNER), f),
        "w_down": nrm(ks[18], (N_A, INNER, D_MODEL), f) * (INNER ** -0.5 * BETA),
        "w_cin": nrm(ks[19], (N_B, D_MODEL, 3 * CONV_CH), f) * D_MODEL ** -0.5,
        "b_cin": 0.01 * nrm(ks[20], (N_B, 3 * CONV_CH), f),
        "w_dw": nrm(ks[21], (N_B, CONV_W, CONV_CH), f) * CONV_W ** -0.5,
        "b_dw": 0.01 * nrm(ks[22], (N_B, CONV_CH), f),
        "cln_g": 1.0 + 0.01 * nrm(ks[23], (N_B, CONV_CH), f),
        "cln_b": 0.01 * nrm(ks[24], (N_B, CONV_CH), f),
        "w_cout": nrm(ks[25], (N_B, CONV_CH, D_MODEL), f) * (CONV_CH ** -0.5 * BETA),
        "b_cout": 0.01 * nrm(ks[26], (N_B, D_MODEL), f),
        "post_ln_g": 1.0 + 0.01 * nrm(ks[27], (DEPTH, D_MODEL), f),
        "post_ln_b": 0.01 * nrm(ks[28], (DEPTH, D_MODEL), f),
    }


def reference(x_prompt, x_sample, state_mlstm_C, state_mlstm_n, state_mlstm_m, state_mlstm_conv, state_conformer_conv,
              w_up, w_mconv, b_mconv, w_q, w_k, w_v, w_gate, b_gate, mh_gain, skip, w_down,
              w_cin, b_cin, w_dw, b_dw, cln_g, cln_b, w_cout, b_cout, post_ln_g, post_ln_b):
    params = (w_up, w_mconv, b_mconv, w_q, w_k, w_v, w_gate, b_gate, mh_gain, skip, w_down,
              w_cin, b_cin, w_dw, b_dw, cln_g, cln_b, w_cout, b_cout, post_ln_g, post_ln_b)
    B = x_prompt.shape[0]
    f32 = jnp.float32
    z_C = jnp.zeros((N_A, B, N_HEADS, DK, DV), f32)
    z_n = jnp.zeros((N_A, B, N_HEADS, DK), f32)
    z_m = jnp.zeros((N_A, B, N_HEADS), f32)
    z_mc = jnp.zeros((N_A, B, MCONV_W - 1, INNER), x_prompt.dtype)
    z_cc = jnp.zeros((N_B, B, CONV_W - 1, CONV_CH), x_prompt.dtype)
    y_prompt, p_C, p_n, p_m, p_mc, p_cc = _trunk(x_prompt, z_C, z_n, z_m, z_mc, z_cc, *params)
    y_sample, s_C, s_n, s_m, s_mc, s_cc = _trunk(x_sample, state_mlstm_C, state_mlstm_n, state_mlstm_m,
                                                 state_mlstm_conv, state_conformer_conv, *params)
    return (y_prompt, y_sample, p_C, p_n, p_m, p_mc, p_cc, s_C, s_n, s_m, s_mc, s_cc)
```

```python
import functools

import jax
import jax.numpy as jnp
from jax import lax
from jax.experimental import pallas as pl
from jax.experimental.pallas import tpu as pltpu

LN_EPS = 1e-5
F32 = jnp.float32
BF16 = jnp.bfloat16

VMEM_LIMIT_BYTES = 58 * 1024 * 1024
MXU_DIM = 256
HALO = 16
CHALO = 32


def _params(*sem):
    return pltpu.CompilerParams(dimension_semantics=sem, vmem_limit_bytes=VMEM_LIMIT_BYTES)


def _tile(n, pref):
    t = min(n, pref)
    assert n % t == 0, (n, pref)
    return t


def _sigmoid(x):
    return jax.nn.sigmoid(x)


def _mm_kernel(x_ref, w_ref, o_ref):
    o_ref[...] = jnp.dot(x_ref[...], w_ref[...], preferred_element_type=F32).astype(o_ref.dtype)


def _matmul(x, w, out_dtype, bm, bn):
    M, K = x.shape
    N = w.shape[1]
    bm, bn = _tile(M, bm), _tile(N, bn)
    return pl.pallas_call(
        _mm_kernel,
        grid=(M // bm, N // bn),
        in_specs=[pl.BlockSpec((bm, K), lambda i, j: (i, 0)),
                  pl.BlockSpec((K, bn), lambda i, j: (0, j))],
        out_specs=pl.BlockSpec((bm, bn), lambda i, j: (i, j)),
        out_shape=jax.ShapeDtypeStruct((M, N), out_dtype),
        compiler_params=_params("parallel", "arbitrary"),
        name="up_proj",
    )(x, w)


def _mm_ln_kernel(alpha, x_ref, w_ref, b_ref, r_ref, g_ref, be_ref, o_ref, acc_ref):
    n = pl.program_id(1)
    nn = acc_ref.shape[0]
    bn = acc_ref.shape[2]
    acc = jnp.dot(x_ref[...], w_ref[...], preferred_element_type=F32) + b_ref[...]
    acc_ref[n] = alpha * r_ref[...] + acc

    @pl.when(n == nn - 1)
    def _():
        width = nn * bn
        tot = acc_ref[0].sum(axis=1, keepdims=True)
        for j in range(1, nn):
            tot = tot + acc_ref[j].sum(axis=1, keepdims=True)
        mu = tot / width
        sq = jnp.square(acc_ref[0] - mu).sum(axis=1, keepdims=True)
        for j in range(1, nn):
            sq = sq + jnp.square(acc_ref[j] - mu).sum(axis=1, keepdims=True)
        rstd = lax.rsqrt(sq / width + LN_EPS)
        for j in range(nn):
            sl = slice(j * bn, (j + 1) * bn)
            o_ref[:, sl] = (acc_ref[j] - mu) * rstd * g_ref[:, sl] + be_ref[:, sl]


def _matmul_ln(x, w, b, r, g, be, alpha, bm, bn, name):
    M, K = x.shape
    N = w.shape[1]
    bm, bn = _tile(M, bm), _tile(N, bn)
    return pl.pallas_call(
        functools.partial(_mm_ln_kernel, alpha),
        grid=(M // bm, N // bn),
        in_specs=[pl.BlockSpec((bm, K), lambda i, j: (i, 0)),
                  pl.BlockSpec((K, bn), lambda i, j: (0, j)),
                  pl.BlockSpec((1, bn), lambda i, j: (0, j)),
                  pl.BlockSpec((bm, bn), lambda i, j: (i, j)),
                  pl.BlockSpec((1, N), lambda i, j: (0, 0)),
                  pl.BlockSpec((1, N), lambda i, j: (0, 0))],
        out_specs=pl.BlockSpec((bm, N), lambda i, j: (i, 0)),
        out_shape=jax.ShapeDtypeStruct((M, N), F32),
        scratch_shapes=[pltpu.VMEM((N // bn, bm, bn), F32)],
        compiler_params=_params("parallel", "arbitrary"),
        name=name,
    )(x, w, b.reshape(1, N), r, g.reshape(1, N), be.reshape(1, N))


def _conv4_silu(xp_ref, wc_ref, bc_ref, rows):
    taps = wc_ref.shape[0]
    base = HALO - (taps - 1)
    acc = xp_ref[..., base:base + rows, :] * wc_ref[0:1, :]
    for j in range(1, taps):
        acc = acc + xp_ref[..., base + j:base + j + rows, :] * wc_ref[j:j + 1, :]
    xc = acc + bc_ref[...]
    return xc * _sigmoid(xc)


def _headwise(xb, w_ref):
    nt = w_ref.shape[0]
    outs = [jnp.dot(xb[:, i * MXU_DIM:(i + 1) * MXU_DIM], w_ref[i], preferred_element_type=F32)
            for i in range(nt)]
    return outs[0] if nt == 1 else jnp.concatenate(outs, axis=1)


def _gates_kernel(nheads, x_ref, halo_ref, hist_ref, wc_ref, bc_ref, wq_ref, wk_ref, wv_ref,
                  wg_ref, bg_ref, o_ref, xp_ref):
    t = pl.program_id(1)
    s = pl.program_id(2)
    ns = pl.num_programs(2)
    SB, TB, W = x_ref.shape
    x = x_ref[...]
    xp_ref[:, 0:HALO, :] = jnp.where(t == 0, hist_ref[...], halo_ref[...].astype(F32))
    xp_ref[:, HALO:, :] = x.astype(F32)
    xa = _conv4_silu(xp_ref, wc_ref, bc_ref, TB)
    xab = xa.reshape(SB * TB, W).astype(BF16)
    xb = x.reshape(SB * TB, W)
    q = _headwise(xab, wq_ref).astype(BF16)
    k = _headwise(xab, wk_ref).astype(BF16)
    v = _headwise(xb, wv_ref).astype(BF16)
    part = (jnp.dot(q, wg_ref[0], preferred_element_type=F32)
            + jnp.dot(k, wg_ref[1], preferred_element_type=F32)
            + jnp.dot(v, wg_ref[2], preferred_element_type=F32)).reshape(SB, TB, 2 * nheads)

    @pl.when(s == 0)
    def _():
        o_ref[...] = part

    @pl.when(s > 0)
    def _():
        o_ref[...] += part

    @pl.when(s == ns - 1)
    def _():
        g = o_ref[...] + bg_ref[...]
        lane = lax.broadcasted_iota(jnp.int32, g.shape, 2)
        logsig = jnp.minimum(g, 0.0) - jnp.log1p(jnp.exp(-jnp.abs(g)))
        o_ref[...] = jnp.where(lane < nheads, g, logsig)


def _gates(up, hist, wc, bc, wq, wk, wv, wg, bg, sb, tb, inner, nheads):
    B, T, _ = up.shape
    W = inner // nheads
    nt = W // MXU_DIM
    tbh = tb // HALO
    return pl.pallas_call(
        functools.partial(_gates_kernel, nheads),
        grid=(B // sb, T // tb, nheads),
        in_specs=[pl.BlockSpec((sb, tb, W), lambda b, t, s: (b, t, s)),
                  pl.BlockSpec((sb, HALO, W), lambda b, t, s: (b, jnp.maximum(t * tbh - 1, 0), s)),
                  pl.BlockSpec((sb, HALO, W), lambda b, t, s: (b, 0, s)),
                  pl.BlockSpec((wc.shape[0], W), lambda b, t, s: (0, s)),
                  pl.BlockSpec((1, W), lambda b, t, s: (0, s)),
                  pl.BlockSpec((nt, MXU_DIM, MXU_DIM), lambda b, t, s: (s, 0, 0)),
                  pl.BlockSpec((nt, MXU_DIM, MXU_DIM), lambda b, t, s: (s, 0, 0)),
                  pl.BlockSpec((nt, MXU_DIM, MXU_DIM), lambda b, t, s: (s, 0, 0)),
                  pl.BlockSpec((3, W, 2 * nheads), lambda b, t, s: (0, s, 0)),
                  pl.BlockSpec((1, 1, 2 * nheads), lambda b, t, s: (0, 0, 0))],
        out_specs=pl.BlockSpec((sb, tb, 2 * nheads), lambda b, t, s: (b, t, 0)),
        out_shape=jax.ShapeDtypeStruct((B, T, 2 * nheads), F32),
        scratch_shapes=[pltpu.VMEM((sb, tb + HALO, W), F32)],
        compiler_params=_params("parallel", "arbitrary", "arbitrary"),
        name="mlstm_gates",
    )(up, up, hist, wc, bc, wq, wk, wv, wg, bg)


def _cell_kernel(has_state, nheads, *refs):
    if has_state:
        (xm_ref, z_ref, g_ref, hist_ref, wc_ref, bc_ref, wq_ref, wk_ref, wv_ref, gain_ref, skip_ref,
         c0_ref, n0_ref, m0_ref, o_ref, co_ref, no_ref, mo_ref, xp_ref) = refs
    else:
        (xm_ref, z_ref, g_ref, hist_ref, wc_ref, bc_ref, wq_ref, wk_ref, wv_ref, gain_ref, skip_ref,
         o_ref, co_ref, no_ref, mo_ref, xp_ref) = refs
    head = pl.program_id(1)
    c = pl.program_id(2)
    L, DK = xm_ref.shape[1], xm_ref.shape[2]

    @pl.when(c == 0)
    def _():
        xp_ref[0:HALO, :] = hist_ref[0]
        if has_state:
            co_ref[0, 0] = c0_ref[0, 0]
            no_ref[0, 0] = n0_ref[0, 0]
            mo_ref[0, 0] = m0_ref[0, 0]
        else:
            co_ref[...] = jnp.zeros_like(co_ref)
            no_ref[...] = jnp.zeros_like(no_ref)
            mo_ref[...] = jnp.zeros_like(mo_ref)

    xb = xm_ref[0]
    xp_ref[HALO:, :] = xb.astype(F32)
    xa = _conv4_silu(xp_ref, wc_ref, bc_ref, L)
    xp_ref[0:HALO, :] = xp_ref[L:L + HALO, :]
    xab = xa.astype(BF16)
    q = _headwise(xab, wq_ref)
    ks = _headwise(xab, wk_ref) * (DK ** -0.5)
    qb = q.astype(BF16)
    kb = ks.astype(BF16)
    vb = _headwise(xb, wv_ref).astype(BF16)

    g = g_ref[0]
    lane = lax.broadcasted_iota(jnp.int32, g.shape, 1)
    ig_col = jnp.sum(jnp.where(lane == head, g, 0.0), axis=1, keepdims=True)
    lf_col = jnp.sum(jnp.where(lane == head + nheads, g, 0.0), axis=1, keepdims=True)
    row = lax.broadcasted_iota(jnp.int32, (L, L), 0)
    col = lax.broadcasted_iota(jnp.int32, (L, L), 1)
    eye = row == col
    causal = col <= row
    lf_row = jnp.sum(jnp.where(eye, lf_col, 0.0), axis=0, keepdims=True)
    ig_row = jnp.sum(jnp.where(eye, ig_col, 0.0), axis=0, keepdims=True)
    b_col = jnp.sum(jnp.where(causal, lf_row, 0.0), axis=1, keepdims=True)
    b_row = jnp.sum(jnp.where(row <= col, lf_col, 0.0), axis=0, keepdims=True)

    m_prev = mo_ref[0, 0]
    log_d = jnp.where(causal, b_col - b_row + ig_row, -jnp.inf)
    g_col = b_col + m_prev
    m_t = jnp.maximum(g_col, jnp.max(log_d, axis=1, keepdims=True))
    d_m = jnp.exp(log_d - m_t)
    inter = jnp.exp(g_col - m_t)

    s_mat = lax.dot_general(qb, kb, (((1,), (1,)), ((), ())), preferred_element_type=F32) * d_m
    c_old = co_ref[0, 0]
    n_old = no_ref[0, 0]
    q_c = jnp.dot(qb, c_old.astype(BF16), preferred_element_type=F32)
    num = jnp.dot(s_mat.astype(BF16), vb, preferred_element_type=F32) + inter * q_c
    den = jnp.sum(s_mat, axis=1, keepdims=True) + inter * jnp.sum(q * n_old, axis=1, keepdims=True)
    h = num / jnp.maximum(jnp.abs(den), jnp.exp(-m_t))

    b_last = b_col[L - 1:L, :]
    logw = b_last - b_col + ig_col
    decay = b_last + m_prev
    m_new = jnp.maximum(decay, jnp.max(logw, axis=0, keepdims=True))
    wk = ks * jnp.exp(logw - m_new)
    sc = jnp.exp(decay - m_new)
    kv = lax.dot_general(wk.astype(BF16), vb, (((0,), (0,)), ((), ())), preferred_element_type=F32)
    co_ref[0, 0] = sc * c_old + kv
    no_ref[0, 0] = sc * n_old + jnp.sum(wk, axis=0, keepdims=True)
    mo_ref[0, 0] = m_new

    mu = jnp.mean(h, axis=1, keepdims=True)
    var = jnp.mean(jnp.square(h - mu), axis=1, keepdims=True)
    hn = (h - mu) * lax.rsqrt(var + LN_EPS) * gain_ref[...]
    zf = z_ref[0].astype(F32)
    o_ref[0] = ((hn + skip_ref[...] * xa) * (zf * _sigmoid(zf))).astype(o_ref.dtype)


def _cell(up, gates, hist, wc, bc, wq, wk, wv, gain, skip, state, chunk, inner, nheads):
    B, T, _ = up.shape
    DK = inner // nheads
    nt = DK // MXU_DIM
    L = _tile(T, chunk)
    has_state = state is not None
    in_specs = [pl.BlockSpec((1, L, DK), lambda b, h, c: (b, c, h)),
                pl.BlockSpec((1, L, DK), lambda b, h, c: (b, c, nheads + h)),
                pl.BlockSpec((1, L, 2 * nheads), lambda b, h, c: (b, c, 0)),
                pl.BlockSpec((1, HALO, DK), lambda b, h, c: (b, 0, h)),
                pl.BlockSpec((wc.shape[0], DK), lambda b, h, c: (0, h)),
                pl.BlockSpec((1, DK), lambda b, h, c: (0, h)),
                pl.BlockSpec((nt, MXU_DIM, MXU_DIM), lambda b, h, c: (h, 0, 0)),
                pl.BlockSpec((nt, MXU_DIM, MXU_DIM), lambda b, h, c: (h, 0, 0)),
                pl.BlockSpec((nt, MXU_DIM, MXU_DIM), lambda b, h, c: (h, 0, 0)),
                pl.BlockSpec((1, DK), lambda b, h, c: (0, h)),
                pl.BlockSpec((1, DK), lambda b, h, c: (0, h))]
    args = [up, up, gates, hist, wc, bc, wq, wk, wv, gain, skip]
    state_specs = [pl.BlockSpec((1, 1, DK, DK), lambda b, h, c: (b, h, 0, 0)),
                   pl.BlockSpec((1, 1, 1, DK), lambda b, h, c: (b, h, 0, 0)),
                   pl.BlockSpec((1, 1, 1, 1), lambda b, h, c: (b, h, 0, 0))]
    if has_state:
        c0, n0, m0 = state
        in_specs += state_specs
        args += [c0, n0.reshape(B, nheads, 1, DK), m0.reshape(B, nheads, 1, 1)]
    out, c_new, n_new, m_new = pl.pallas_call(
        functools.partial(_cell_kernel, has_state, nheads),
        grid=(B, nheads, T // L),
        in_specs=in_specs,
        out_specs=[pl.BlockSpec((1, L, DK), lambda b, h, c: (b, c, h))] + state_specs,
        out_shape=[jax.ShapeDtypeStruct((B, T, inner), BF16),
                   jax.ShapeDtypeStruct((B, nheads, DK, DK), F32),
                   jax.ShapeDtypeStruct((B, nheads, 1, DK), F32),
                   jax.ShapeDtypeStruct((B, nheads, 1, 1), F32)],
        scratch_shapes=[pltpu.VMEM((L + HALO, DK), F32)],
        compiler_params=_params("parallel", "parallel", "arbitrary"),
        name="mlstm_cell",
    )(*args)
    return out, c_new, n_new.reshape(B, nheads, DK), m_new.reshape(B, nheads)


def _cin_kernel(x_ref, wa_ref, wg_ref, wz_ref, ba_ref, bg_ref, bz_ref, u_ref, sz_ref):
    x = x_ref[...]
    a = jnp.dot(x, wa_ref[...], preferred_element_type=F32) + ba_ref[...]
    gl = jnp.dot(x, wg_ref[...], preferred_element_type=F32) + bg_ref[...]
    zg = jnp.dot(x, wz_ref[...], preferred_element_type=F32) + bz_ref[...]
    u_ref[...] = a * _sigmoid(gl)
    sz_ref[...] = (zg * _sigmoid(zg)).astype(sz_ref.dtype)


def _cin_glu(x, w, b, bm, bn):
    M, K = x.shape
    C = w.shape[1] // 3
    bm, bn = _tile(M, bm), _tile(C, bn)
    nb = C // bn
    b2 = b.reshape(1, 3 * C)
    wspec = lambda off: pl.BlockSpec((K, bn), lambda i, j: (0, j + off))
    bspec = lambda off: pl.BlockSpec((1, bn), lambda i, j: (0, j + off))
    return pl.pallas_call(
        _cin_kernel,
        grid=(M // bm, nb),
        in_specs=[pl.BlockSpec((bm, K), lambda i, j: (i, 0)),
                  wspec(0), wspec(nb), wspec(2 * nb), bspec(0), bspec(nb), bspec(2 * nb)],
        out_specs=[pl.BlockSpec((bm, bn), lambda i, j: (i, j)),
                   pl.BlockSpec((bm, bn), lambda i, j: (i, j))],
        out_shape=[jax.ShapeDtypeStruct((M, C), F32), jax.ShapeDtypeStruct((M, C), BF16)],
        compiler_params=_params("parallel", "arbitrary"),
        name="conv_in_glu",
    )(x, w, w, w, b2, b2, b2)


def _dwconv_kernel(u_ref, halo_ref, hist_ref, w_ref, b_ref, o_ref, xp_ref):
    t = pl.program_id(1)
    SB, TB, W = u_ref.shape
    taps = w_ref.shape[0]
    base = CHALO - (taps - 1)
    xp_ref[:, 0:CHALO, :] = jnp.where(t == 0, hist_ref[...], halo_ref[...])
    xp_ref[:, CHALO:, :] = u_ref[...]
    lanes = 128
    for i in range(W // lanes):
        sl = slice(i * lanes, (i + 1) * lanes)
        acc = xp_ref[:, base:base + TB, sl] * w_ref[0:1, sl]
        for j in range(1, taps):
            acc = acc + xp_ref[:, base + j:base + j + TB, sl] * w_ref[j:j + 1, sl]
        o_ref[:, :, sl] = acc + b_ref[:, sl]


def _dwconv(u, hist, w, b, sb, tb, cw):
    B, T, C = u.shape
    cw = _tile(C, cw)
    tbh = tb // CHALO
    return pl.pallas_call(
        _dwconv_kernel,
        grid=(B // sb, T // tb, C // cw),
        in_specs=[pl.BlockSpec((sb, tb, cw), lambda b, t, s: (b, t, s)),
                  pl.BlockSpec((sb, CHALO, cw), lambda b, t, s: (b, jnp.maximum(t * tbh - 1, 0), s)),
                  pl.BlockSpec((sb, CHALO, cw), lambda b, t, s: (b, 0, s)),
                  pl.BlockSpec((w.shape[0], cw), lambda b, t, s: (0, s)),
                  pl.BlockSpec((1, cw), lambda b, t, s: (0, s))],
        out_specs=pl.BlockSpec((sb, tb, cw), lambda b, t, s: (b, t, s)),
        out_shape=jax.ShapeDtypeStruct((B, T, C), F32),
        scratch_shapes=[pltpu.VMEM((sb, tb + CHALO, cw), F32)],
        compiler_params=_params("parallel", "parallel", "parallel"),
        name="dwconv31",
    )(u, u, hist, w, b)


def _cnorm_kernel(c_ref, sz_ref, g_ref, b_ref, o_ref):
    c = c_ref[...]
    mu = jnp.mean(c, axis=1, keepdims=True)
    var = jnp.mean(jnp.square(c - mu), axis=1, keepdims=True)
    y = (c - mu) * lax.rsqrt(var + LN_EPS) * g_ref[...] + b_ref[...]
    o_ref[...] = (y * _sigmoid(y) * sz_ref[...].astype(F32)).astype(o_ref.dtype)


def _cnorm_gate(c, sz, g, b, bm):
    M, C = c.shape
    bm = _tile(M, bm)
    return pl.pallas_call(
        _cnorm_kernel,
        grid=(M // bm,),
        in_specs=[pl.BlockSpec((bm, C), lambda i: (i, 0)),
                  pl.BlockSpec((bm, C), lambda i: (i, 0)),
                  pl.BlockSpec((1, C), lambda i: (0, 0)),
                  pl.BlockSpec((1, C), lambda i: (0, 0))],
        out_specs=pl.BlockSpec((bm, C), lambda i: (i, 0)),
        out_shape=jax.ShapeDtypeStruct((M, C), BF16),
        compiler_params=_params("parallel"),
        name="chan_norm_gate",
    )(c, sz, g.reshape(1, C), b.reshape(1, C))


def _blockdiag(w):
    nb, bs, _ = w.shape
    per = MXU_DIM // bs
    w4 = w.reshape(nb // per, per, bs, bs)
    eye = jnp.eye(per, dtype=w.dtype)
    dense = w4[:, :, :, None, :] * eye[None, :, None, :, None]
    return dense.reshape(nb // per, MXU_DIM, MXU_DIM).astype(BF16)


def _pad_hist(hist, rows):
    B, r, C = hist.shape
    return jnp.concatenate([jnp.zeros((B, rows - r, C), F32), hist.astype(F32)], axis=1)


def _mlstm_layer(x, state, hist, p, post_g, post_b, alpha, cfg):
    B, T, D = x.shape
    inner = p["w_mconv"].shape[1]
    nheads = p["b_gate"].shape[0] // 2
    M = B * T
    xb = x.reshape(M, D).astype(BF16)
    up = _matmul(xb, p["w_up"], BF16, cfg["bm"], 1024).reshape(B, T, 2 * inner)
    hist_p = _pad_hist(hist, HALO)
    wc = p["w_mconv"].astype(F32)
    bc = p["b_mconv"].reshape(1, inner).astype(F32)
    gates = _gates(up, hist_p, wc, bc, p["wq"], p["wk"], p["wv"], p["w_gate"],
                   p["b_gate"].reshape(1, 1, 2 * nheads).astype(F32),
                   cfg["sb"], cfg["tb"], inner, nheads)
    pre, c_new, n_new, m_new = _cell(up, gates, hist_p, wc, bc, p["wq"], p["wk"], p["wv"],
                                     p["mh_gain"].reshape(1, inner).astype(F32),
                                     p["skip"].reshape(1, inner).astype(F32),
                                     state, cfg["chunk"], inner, nheads)
    taps = wc.shape[0]
    hist_new = up[:, T - (taps - 1):, :inner].astype(F32)
    y = _matmul_ln(pre.reshape(M, inner), p["w_down"], jnp.zeros((D,), F32), x.reshape(M, D),
                   post_g, post_b, alpha, 512, 256, "down_proj_norm")
    return y.reshape(B, T, D), c_new, n_new, m_new, hist_new


def _conv_layer(x, hist, p, post_g, post_b, alpha, cfg):
    B, T, D = x.shape
    M = B * T
    C = p["w_dw"].shape[1]
    taps = p["w_dw"].shape[0]
    xb = x.reshape(M, D).astype(BF16)
    u, sz = _cin_glu(xb, p["w_cin"], p["b_cin"].astype(F32), cfg["bm"], 512)
    u3 = u.reshape(B, T, C)
    c = _dwconv(u3, _pad_hist(hist, CHALO), p["w_dw"].astype(F32), p["b_dw"].reshape(1, C).astype(F32),
                cfg["sb"], cfg["tbc"], 512)
    pre = _cnorm_gate(c.reshape(M, C), sz, p["cln_g"].astype(F32), p["cln_b"].astype(F32), 256)
    y = _matmul_ln(pre, p["w_cout"], p["b_cout"].astype(F32), x.reshape(M, D),
                   post_g, post_b, alpha, 512, 256, "conv_out_norm")
    hist_new = u3[:, T - (taps - 1):, :]
    return y.reshape(B, T, D), hist_new


def _stack(xs):
    return xs[0][None] if len(xs) == 1 else jnp.stack(xs)


def _trunk(x, st_c, st_n, st_m, st_mconv, st_cconv, layers, post_ln_g, post_ln_b, cfg):
    depth = post_ln_g.shape[0]
    alpha = (2 * depth) ** 0.25
    cs, ns, ms, mcs, ccs = [], [], [], [], []
    ia = ib = 0
    for i in range(depth):
        if i % 2 == 0:
            state = None if st_c is None else (st_c[ia], st_n[ia], st_m[ia])
            x, c, n, m, hc = _mlstm_layer(x, state, st_mconv[ia], layers[i], post_ln_g[i], post_ln_b[i],
                                          alpha, cfg)
            cs.append(c); ns.append(n); ms.append(m); mcs.append(hc)
            ia += 1
        else:
            x, hc = _conv_layer(x, st_cconv[ib], layers[i], post_ln_g[i], post_ln_b[i], alpha, cfg)
            ccs.append(hc)
            ib += 1
    return (x, _stack(cs), _stack(ns), _stack(ms), _stack(mcs), _stack(ccs))


def kernel(x_prompt, x_sample, state_mlstm_C, state_mlstm_n, state_mlstm_m, state_mlstm_conv, state_conformer_conv, w_up, w_mconv, b_mconv, w_q, w_k, w_v, w_gate, b_gate, mh_gain, skip, w_down, w_cin, b_cin, w_dw, b_dw, cln_g, cln_b, w_cout, b_cout, post_ln_g, post_ln_b):
    depth = post_ln_g.shape[0]
    layers = []
    for i in range(depth):
        j = i // 2
        if i % 2 == 0:
            layers.append(dict(w_up=w_up[j].astype(BF16), w_mconv=w_mconv[j], b_mconv=b_mconv[j],
                               wq=_blockdiag(w_q[j]), wk=_blockdiag(w_k[j]), wv=_blockdiag(w_v[j]),
                               w_gate=w_gate[j].astype(BF16), b_gate=b_gate[j], mh_gain=mh_gain[j],
                               skip=skip[j], w_down=w_down[j].astype(BF16)))
        else:
            layers.append(dict(w_cin=w_cin[j].astype(BF16), b_cin=b_cin[j], w_dw=w_dw[j], b_dw=b_dw[j],
                               cln_g=cln_g[j], cln_b=cln_b[j], w_cout=w_cout[j].astype(BF16),
                               b_cout=b_cout[j]))
    B, T, D = x_prompt.shape
    SB, ST, _ = x_sample.shape
    n_a, n_b = w_up.shape[0], w_cin.shape[0]
    inner = w_mconv.shape[2]
    conv_ch = w_dw.shape[2]
    zero_mconv = jnp.zeros((n_a, B, w_mconv.shape[1] - 1, inner), F32)
    zero_cconv = jnp.zeros((n_b, B, w_dw.shape[1] - 1, conv_ch), F32)
    cfg_p = dict(bm=1024, sb=1, tb=_tile(T, 512), tbc=_tile(T, 256), chunk=256)
    cfg_s = dict(bm=SB * ST, sb=SB, tb=ST, tbc=ST, chunk=ST)
    out_p = _trunk(x_prompt, None, None, None, zero_mconv, zero_cconv, layers, post_ln_g, post_ln_b, cfg_p)
    out_s = _trunk(x_sample, state_mlstm_C, state_mlstm_n, state_mlstm_m, state_mlstm_conv,
                   state_conformer_conv, layers, post_ln_g, post_ln_b, cfg_s)
    return (out_p[0], out_s[0]) + out_p[1:] + out_s[1:]
```
